```python
import jax, jax.numpy as jnp
from jax import lax
import numpy as np

D_MODEL = 1024
BATCH = 16
SEQ = 4096
DEPTH = 1

HEAD_DIM = 64
ATTN_WIDTH = D_MODEL // 2
POOL_WIDTH = D_MODEL - ATTN_WIDTH
MIX_WIDTH = ATTN_WIDTH + POOL_WIDTH
N_Q_HEADS = ATTN_WIDTH // HEAD_DIM
GQA_GROUP = 4
N_KV_HEADS = N_Q_HEADS // GQA_GROUP
KV_WIDTH = N_KV_HEADS * HEAD_DIM
IN_PROJ_WIDTH = ATTN_WIDTH + 2 * KV_WIDTH + POOL_WIDTH
WINDOW = 128
ATTN_BLOCK = 128
ROPE_THETA = 500000.0
ROPE_DIM = HEAD_DIM // 4
POOL_WINDOWS = (2, 4, 8, 16)
N_POOL_GROUPS = len(POOL_WINDOWS)
POOL_GROUP_WIDTH = POOL_WIDTH // N_POOL_GROUPS
N_EXPERTS = 32
TOP_K = 4
D_FF = D_MODEL
SWIGLU_LIMIT = 7.0
SWIGLU_ALPHA = 1.702
EXPERT_BLOCK = 256
NORM_EPS = 1e-5
QK_NORM_EPS = 1e-6
NEG_INF = -1e30

kernel_name = 'hymba_swa_sink_pool_moe_layer'


def rms_norm(x, w, eps):
    xf = x.astype(jnp.float32)
    y = xf * lax.rsqrt(jnp.mean(xf * xf, axis=-1, keepdims=True) + eps)
    return (y * w.astype(jnp.float32)).astype(x.dtype)


def partial_rope(x, cos, sin):
    half = ROPE_DIM // 2
    r = x[..., :ROPE_DIM].astype(jnp.float32)
    x1, x2 = r[..., :half], r[..., half:]
    rot = jnp.concatenate([x1 * cos - x2 * sin, x2 * cos + x1 * sin], axis=-1).astype(x.dtype)
    return jnp.concatenate([rot, x[..., ROPE_DIM:]], axis=-1)


def sliding_window_attention(q, k, v, sinks):
    B, S, Hkv, G, D = q.shape
    n_blocks = S // ATTN_BLOCK
    pad = ((0, 0), (ATTN_BLOCK, 0), (0, 0), (0, 0))
    kp = jnp.pad(k, pad)
    vp = jnp.pad(v, pad)
    qi = jnp.arange(ATTN_BLOCK)[:, None]
    kj = jnp.arange(2 * ATTN_BLOCK)[None, :]
    band = (kj <= qi + ATTN_BLOCK) & (kj > qi + ATTN_BLOCK - WINDOW)
    sink = sinks.astype(jnp.float32).reshape(1, Hkv, G, 1, 1)
    scale = HEAD_DIM ** -0.5

    def one_block(i):
        start = i * ATTN_BLOCK
        qb = lax.dynamic_slice_in_dim(q, start, ATTN_BLOCK, axis=1)
        kb = lax.dynamic_slice_in_dim(kp, start, 2 * ATTN_BLOCK, axis=1)
        vb = lax.dynamic_slice_in_dim(vp, start, 2 * ATTN_BLOCK, axis=1)
        s = jnp.einsum('bqhgd,bkhd->bhgqk', qb, kb).astype(jnp.float32) * scale
        valid = band & (start - ATTN_BLOCK + kj >= 0)
        s = jnp.where(valid, s, NEG_INF)
        sink_col = jnp.broadcast_to(sink, s.shape[:-1] + (1,))
        probs = jax.nn.softmax(jnp.concatenate([s, sink_col], axis=-1), axis=-1)[..., :-1]
        return jnp.einsum('bhgqk,bkhd->bqhgd', probs.astype(vb.dtype), vb)

    out = lax.map(one_block, jnp.arange(n_blocks))
    out = jnp.moveaxis(out, 0, 1)
    return out.reshape(B, S, Hkv * G * D)


def multiscale_pool(u, pool_w, pool_scale):
    B, S, _ = u.shape
    uf = u.astype(jnp.float32).reshape(B, S, N_POOL_GROUPS, POOL_GROUP_WIDTH)
    csum = jnp.cumsum(uf, axis=1)
    pos1 = jnp.arange(S) + 1
    groups = []
    for g, w in enumerate(POOL_WINDOWS):
        c = csum[:, :, g]
        lagged = jnp.pad(c[:, :S - w], ((0, 0), (w, 0), (0, 0)))
        count = jnp.minimum(pos1, w).astype(jnp.float32)[None, :, None]
        groups.append((c - lagged) / count - uf[:, :, g])
    pooled = jnp.stack(groups, axis=2)
    mixed = jnp.einsum('bsgc,gcd->bsgd', pooled, pool_w.astype(jnp.float32))
    return (mixed.reshape(B, S, POOL_WIDTH) * pool_scale.astype(jnp.float32)).astype(u.dtype)


def moe_ffn(h, w_router, b_router, w_gu, b_gu, w_down, b_down):
    T = h.shape[0]
    n_assign = T * TOP_K
    logits = (h @ w_router).astype(jnp.float32) + b_router.astype(jnp.float32)
    top_v, top_i = lax.top_k(logits, TOP_K)
    gates = jax.nn.softmax(top_v, axis=-1)
    e_flat = top_i.reshape(-1)
    g_flat = gates.reshape(-1)
    tok_flat = jnp.arange(n_assign, dtype=jnp.int32) // TOP_K
    order = jnp.argsort(e_flat)
    e_sorted = e_flat[order]
    counts = jnp.bincount(e_flat, length=N_EXPERTS)
    start = jnp.cumsum(counts) - counts
    padded = ((counts + EXPERT_BLOCK - 1) // EXPERT_BLOCK) * EXPERT_BLOCK
    pend = jnp.cumsum(padded)
    pstart = pend - padded
    slot = pstart[e_sorted] + (jnp.arange(n_assign) - start[e_sorted])
    n_slots = -(-n_assign // EXPERT_BLOCK) * EXPERT_BLOCK + N_EXPERTS * EXPERT_BLOCK
    n_blk = n_slots // EXPERT_BLOCK
    slot_tok = jnp.zeros((n_slots,), jnp.int32).at[slot].set(tok_flat[order])
    slot_gate = jnp.zeros((n_slots,), jnp.float32).at[slot].set(g_flat[order])
    blk_start = jnp.arange(n_blk, dtype=pend.dtype) * EXPERT_BLOCK
    blk_expert = jnp.minimum(jnp.searchsorted(pend, blk_start, side='right'), N_EXPERTS - 1)

    def expert_block(args):
        e, toks, gw = args
        xb = h[toks]
        gu = xb @ w_gu[e] + b_gu[e]
        gate = jnp.minimum(gu[:, :D_FF], SWIGLU_LIMIT)
        up = jnp.clip(gu[:, D_FF:], -SWIGLU_LIMIT, SWIGLU_LIMIT)
        act = gate * jax.nn.sigmoid(gate * SWIGLU_ALPHA) * (up + 1.0)
        y = act @ w_down[e] + b_down[e]
        return (y * gw[:, None]).astype(h.dtype)

    y = lax.map(expert_block, (blk_expert,
                               slot_tok.reshape(n_blk, EXPERT_BLOCK),
                               slot_gate.reshape(n_blk, EXPERT_BLOCK)))
    return jax.ops.segment_sum(y.reshape(n_slots, -1), slot_tok, num_segments=T)


def setup_inputs(seed: int = 0) -> dict:
    key = jax.random.key(seed)
    ks = jax.random.split(key, 18)
    f32 = jnp.float32
    nrm = lambda k, shape, s: jax.random.normal(k, shape, f32) * s
    x = jax.random.normal(ks[0], (BATCH, SEQ, D_MODEL), f32)
    offsets = jax.random.randint(ks[1], (BATCH, 1), 0, SEQ, dtype=jnp.int32)
    positions = offsets + jnp.arange(SEQ, dtype=jnp.int32)[None, :]
    return {
        'x': x,
        'positions': positions,
        'norm1_w': 1.0 + nrm(ks[2], (D_MODEL,), 0.02),
        'w_in': nrm(ks[3], (D_MODEL, IN_PROJ_WIDTH), D_MODEL ** -0.5),
        'q_norm_w': 1.0 + nrm(ks[4], (HEAD_DIM,), 0.02),
        'k_norm_w': 1.0 + nrm(ks[5], (HEAD_DIM,), 0.02),
        'sinks': nrm(ks[6], (N_Q_HEADS,), 0.5),
        'pool_w': nrm(ks[7], (N_POOL_GROUPS, POOL_GROUP_WIDTH, POOL_GROUP_WIDTH), POOL_GROUP_WIDTH ** -0.5),
        'pool_scale': 1.0 + nrm(ks[8], (POOL_WIDTH,), 0.1),
        'w_out': nrm(ks[9], (MIX_WIDTH, D_MODEL), MIX_WIDTH ** -0.5),
        'norm2_w': 1.0 + nrm(ks[10], (D_MODEL,), 0.02),
        'w_router': nrm(ks[11], (D_MODEL, N_EXPERTS), D_MODEL ** -0.5),
        'b_router': nrm(ks[12], (N_EXPERTS,), 0.01),
        'w_gu': nrm(ks[13], (N_EXPERTS, D_MODEL, 2 * D_FF), D_MODEL ** -0.5),
        'b_gu': nrm(ks[14], (N_EXPERTS, 2 * D_FF), 0.02),
        'w_down': nrm(ks[15], (N_EXPERTS, D_FF, D_MODEL), D_FF ** -0.5),
        'b_down': nrm(ks[16], (N_EXPERTS, D_MODEL), 0.02),
    }


def reference(x, positions, norm1_w, w_in, q_norm_w, k_norm_w, sinks, pool_w, pool_scale,
              w_out, norm2_w, w_router, b_router, w_gu, b_gu, w_down, b_down):
    B, S, _ = x.shape
    inv_freq = ROPE_THETA ** (-jnp.arange(0, ROPE_DIM, 2, dtype=jnp.float32) / ROPE_DIM)
    ang = positions.astype(jnp.float32)[..., None] * inv_freq
    cos = jnp.cos(ang)[:, :, None, :]
    sin = jnp.sin(ang)[:, :, None, :]

    h = x
    for _ in range(DEPTH):
        n = rms_norm(h, norm1_w, NORM_EPS)
        proj = n @ w_in
        q = proj[..., :ATTN_WIDTH].reshape(B, S, N_Q_HEADS, HEAD_DIM)
        k = proj[..., ATTN_WIDTH:ATTN_WIDTH + KV_WIDTH].reshape(B, S, N_KV_HEADS, HEAD_DIM)
        v = proj[..., ATTN_WIDTH + KV_WIDTH:ATTN_WIDTH + 2 * KV_WIDTH].reshape(B, S, N_KV_HEADS, HEAD_DIM)
        u = proj[..., ATTN_WIDTH + 2 * KV_WIDTH:]
        q = partial_rope(rms_norm(q, q_norm_w, QK_NORM_EPS), cos, sin)
        k = partial_rope(rms_norm(k, k_norm_w, QK_NORM_EPS), cos, sin)
        q = q.reshape(B, S, N_KV_HEADS, GQA_GROUP, HEAD_DIM)
        attn_out = sliding_window_attention(q, k, v, sinks)
        pool_out = multiscale_pool(u, pool_w, pool_scale)
        mixed = jnp.concatenate([attn_out, pool_out.astype(attn_out.dtype)], axis=-1)
        h = h + mixed @ w_out
        n2 = rms_norm(h, norm2_w, NORM_EPS).reshape(B * S, D_MODEL)
        h = h + moe_ffn(n2, w_router, b_router, w_gu, b_gu, w_down, b_down).reshape(B, S, D_MODEL)
    return h
```

```python
import functools

import jax
import jax.numpy as jnp
from jax import lax
from jax.experimental import pallas as pl
from jax.experimental.pallas import tpu as pltpu

F32 = jnp.float32
BF16 = jnp.bfloat16

HEAD_DIM = 64
N_Q_HEADS = 8
N_KV_HEADS = 2
GQA_GROUP = N_Q_HEADS // N_KV_HEADS
ATTN_WIDTH = N_Q_HEADS * HEAD_DIM
KV_WIDTH = N_KV_HEADS * HEAD_DIM
WINDOW = 128
ATTN_BLOCK = 128
ROPE_THETA = 500000.0
ROPE_DIM = HEAD_DIM // 4
ROPE_HALF = ROPE_DIM // 2
POOL_WINDOWS = (2, 4, 8, 16)
POOL_GROUP_WIDTH = 128
POOL_WIDTH = POOL_GROUP_WIDTH * len(POOL_WINDOWS)
N_EXPERTS = 32
TOP_K = 4
SWIGLU_LIMIT = 7.0
SWIGLU_ALPHA = 1.702
NORM_EPS = 1e-5
QK_NORM_EPS = 1e-6
NEG_INF = -1e30

LANES = 128
SUBLANES = 8
VMEM_LIMIT_BYTES = 56 * 1024 * 1024

ROW_TILE = 512
POOL_HALO = 32
EXPERT_BLOCK = 256


def _split_bf16(v):
    hi = v.astype(BF16)
    lo = (v - hi.astype(F32)).astype(BF16)
    return hi, lo


def _in_proj_kernel(x_ref, w1_ref, win_ref, qkw_ref, bd_ref, rc_ref, ra_ref, rb_ref,
                    q_out, kv_out, u_out):
    x = x_ref[...]
    ms = jnp.mean(x * x, axis=-1, keepdims=True)
    n = (x * lax.rsqrt(ms + NORM_EPS) * w1_ref[...]).astype(BF16)
    proj = jnp.dot(n, win_ref[...], preferred_element_type=F32)
    rc = rc_ref[...]
    ra = ra_ref[...]
    rb = rb_ref[...]
    bd = bd_ref[...]
    qk_w = qkw_ref[...]
    n_qk_tiles = (ATTN_WIDTH + 2 * KV_WIDTH) // LANES
    for t in range(0, n_qk_tiles, 2):
        blk = proj[:, t * LANES:(t + 2) * LANES]
        hi, lo = _split_bf16(blk * blk)
        msq = (jnp.dot(hi, bd, preferred_element_type=F32)
               + jnp.dot(lo, bd, preferred_element_type=F32))
        normed = blk * lax.rsqrt(msq + QK_NORM_EPS) * qk_w[:, t * LANES:(t + 2) * LANES]
        for s in range(2):
            v = normed[:, s * LANES:(s + 1) * LANES]
            rot = (v * rc + pltpu.roll(v, LANES - ROPE_HALF, 1) * ra
                   + pltpu.roll(v, ROPE_HALF, 1) * rb).astype(BF16)
            col = (t + s) * LANES
            if col < ATTN_WIDTH:
                q_out[:, col:col + LANES] = rot
            else:
                kv_out[:, col - ATTN_WIDTH:col - ATTN_WIDTH + LANES] = rot
    v0 = ATTN_WIDTH + 2 * KV_WIDTH
    kv_out[:, 2 * KV_WIDTH:] = proj[:, v0:v0 + 2 * KV_WIDTH].astype(BF16)
    u_out[...] = proj[:, v0 + 2 * KV_WIDTH:]


def _mixer_kernel(sinks_ref, q_ref, kv_ref, kvh_ref, u_ref, uh_ref, band_ref, pw_ref, ps_ref, out_ref):
    i = pl.program_id(1)
    first = i == 0
    tq = q_ref.shape[0]
    nsub = tq // ATTN_BLOCK
    kw2 = 2 * KV_WIDTH

    lane = lax.broadcasted_iota(jnp.int32, (ATTN_BLOCK, LANES), 1)
    low_half = lane < HEAD_DIM
    qi = lax.broadcasted_iota(jnp.int32, (ATTN_BLOCK, 2 * ATTN_BLOCK), 0)
    kj = lax.broadcasted_iota(jnp.int32, (ATTN_BLOCK, 2 * ATTN_BLOCK), 1)
    band = (kj <= qi + ATTN_BLOCK) & (kj > qi + ATTN_BLOCK - WINDOW)
    band_first = band & (kj >= jnp.where(first, ATTN_BLOCK, 0))

    for j in range(nsub):
        r0 = j * ATTN_BLOCK
        qj = q_ref[r0:r0 + ATTN_BLOCK, :]
        cur = kv_ref[r0:r0 + ATTN_BLOCK, :]
        prev = kvh_ref[...] if j == 0 else kv_ref[r0 - ATTN_BLOCK:r0, :]
        kvcat = jnp.concatenate([prev, cur], axis=0)
        mask1 = band_first if j == 0 else band
        mask = jnp.concatenate([mask1] * GQA_GROUP, axis=0)
        tiles = []
        for g in range(N_KV_HEADS):
            kg = kvcat[:, g * LANES:(g + 1) * LANES]
            vg = kvcat[:, kw2 + g * LANES:kw2 + (g + 1) * LANES]
            pieces = []
            sink_cols = []
            for hh in range(GQA_GROUP):
                h = g * GQA_GROUP + hh
                qt = qj[:, (h // 2) * LANES:(h // 2 + 1) * LANES]
                keep = low_half if h % 2 == 0 else jnp.logical_not(low_half)
                pieces.append(jnp.where(keep, qt, jnp.zeros_like(qt)))
                sink_cols.append(jnp.full((ATTN_BLOCK, 1), sinks_ref[h], F32))
            qs = jnp.concatenate(pieces, axis=0)
            sink = jnp.concatenate(sink_cols, axis=0)
            s = lax.dot_general(qs, kg, (((1,), (1,)), ((), ())), preferred_element_type=F32)
            s = jnp.where(mask, s, NEG_INF)
            m = jnp.maximum(jnp.max(s, axis=-1, keepdims=True), sink)
            p = jnp.exp(s - m)
            denom = jnp.sum(p, axis=-1, keepdims=True) + jnp.exp(sink - m)
            probs = (p / denom).astype(BF16)
            o = jnp.dot(probs, vg, preferred_element_type=F32)
            for pr in range(GQA_GROUP // 2):
                ev = o[(2 * pr) * ATTN_BLOCK:(2 * pr + 1) * ATTN_BLOCK, :]
                od = o[(2 * pr + 1) * ATTN_BLOCK:(2 * pr + 2) * ATTN_BLOCK, :]
                tiles.append(jnp.where(low_half, ev, od))
        out_ref[r0:r0 + ATTN_BLOCK, 0:ATTN_WIDTH] = jnp.concatenate(tiles, axis=1).astype(BF16)

    uc = u_ref[...]
    uh = jnp.where(first, 0.0, uh_ref[...])
    ext = jnp.concatenate([uh, uc], axis=0)
    ext_hi, ext_lo = _split_bf16(ext)
    pos1 = i * tq + lax.broadcasted_iota(jnp.int32, (tq, 1), 0) + 1
    for g, w in enumerate(POOL_WINDOWS):
        c0 = g * POOL_GROUP_WIDTH
        bm = band_ref[g]
        wsum = (jnp.dot(bm, ext_hi[:, c0:c0 + POOL_GROUP_WIDTH], preferred_element_type=F32)
                + jnp.dot(bm, ext_lo[:, c0:c0 + POOL_GROUP_WIDTH], preferred_element_type=F32))
        count = jnp.minimum(pos1, w).astype(F32)
        pooled = wsum / count - uc[:, c0:c0 + POOL_GROUP_WIDTH]
        mixed = jnp.dot(pooled.astype(BF16), pw_ref[g], preferred_element_type=F32)
        mixed = mixed * ps_ref[:, c0:c0 + POOL_GROUP_WIDTH]
        out_ref[:, ATTN_WIDTH + c0:ATTN_WIDTH + c0 + POOL_GROUP_WIDTH] = mixed.astype(BF16)


def _out_router_kernel(x_ref, mix_ref, wout_ref, w2_ref, wrh_ref, wrl_ref, br_ref, tri_ref,
                       h_out, n2_out, route_out, gate_out, cnt_out, base_scr):
    tm = x_ref.shape[0]

    @pl.when(pl.program_id(0) == 0)
    def _():
        base_scr[...] = jnp.zeros_like(base_scr)

    h = x_ref[...] + jnp.dot(mix_ref[...], wout_ref[...], preferred_element_type=F32)
    h_out[...] = h
    ms = jnp.mean(h * h, axis=-1, keepdims=True)
    n2 = h * lax.rsqrt(ms + NORM_EPS) * w2_ref[...]
    for c in range(n2.shape[1] // LANES):
        n2_out[pl.ds(c, tm, stride=SUBLANES), :] = n2[:, c * LANES:(c + 1) * LANES]

    hi, lo = _split_bf16(n2)
    wrh = wrh_ref[...]
    logits = (jnp.dot(hi, wrh, preferred_element_type=F32)
              + jnp.dot(hi, wrl_ref[...], preferred_element_type=F32)
              + jnp.dot(lo, wrh, preferred_element_type=F32)) + br_ref[...]
    lt = logits.T[0:N_EXPERTS, :]
    rows = lax.broadcasted_iota(jnp.int32, (N_EXPERTS, tm), 0)
    picks, vals, idxs = [], [], []
    work = lt
    for _ in range(TOP_K):
        mx = jnp.max(work, axis=0, keepdims=True)
        idx = jnp.min(jnp.where(work == mx, rows, N_EXPERTS), axis=0, keepdims=True)
        pick = rows == idx
        work = jnp.where(pick, -jnp.inf, work)
        picks.append(pick)
        vals.append(mx)
        idxs.append(idx)
    exps = [jnp.exp(v - vals[0]) for v in vals]
    tot = exps[0] + exps[1] + exps[2] + exps[3]
    gates = [e / tot for e in exps]

    onehot = jnp.zeros((N_EXPERTS, tm), F32)
    for pick in picks:
        onehot = onehot + pick.astype(F32)
    before = jnp.dot(onehot.astype(BF16), tri_ref[...], preferred_element_type=F32)
    base = base_scr[...]
    before = before + jnp.concatenate([base] * (tm // LANES), axis=1)
    ranks = [jnp.sum(jnp.where(pick, before, 0.0), axis=0, keepdims=True).astype(jnp.int32)
             for pick in picks]
    route_out[...] = jnp.concatenate(idxs + ranks, axis=0)
    gate_out[...] = jnp.concatenate(gates + gates, axis=0)
    base = base + jnp.broadcast_to(jnp.sum(onehot, axis=1, keepdims=True), base.shape)
    base_scr[...] = base
    cnt_out[...] = base


def _moe_kernel(bexp_ref, nused_ref, src_ref, srcn_ref, dst_ref, gate_ref, n2_hbm,
                wgu_ref, bgu_ref, wd_ref, bd_ref, out_hbm, xbuf, ybuf, gsem, ssem):
    del bexp_ref
    b = pl.program_id(0)
    nused = nused_ref[0]
    bm = gate_ref.shape[2]
    rows = bm * SUBLANES
    slot = b % 2
    d_model = wgu_ref.shape[1]
    d_ff = wd_ref.shape[1]

    def gather_row(idx_ref, m, s):
        tok = idx_ref[0, 0, m]
        return pltpu.make_async_copy(
            n2_hbm.at[pl.ds(pl.multiple_of(tok * SUBLANES, SUBLANES), SUBLANES), :],
            xbuf.at[s, pl.ds(pl.multiple_of(m * SUBLANES, SUBLANES), SUBLANES), :],
            gsem.at[s])

    def scatter_row(m, s):
        dst = dst_ref[0, 0, m]
        return pltpu.make_async_copy(
            ybuf.at[s, pl.ds(pl.multiple_of(m * SUBLANES, SUBLANES), SUBLANES), :],
            out_hbm.at[pl.ds(pl.multiple_of(dst * SUBLANES, SUBLANES), SUBLANES), :],
            ssem.at[s])

    def gather_all(s):
        return pltpu.make_async_copy(n2_hbm.at[pl.ds(0, rows), :], xbuf.at[s], gsem.at[s])

    def scatter_all(s):
        return pltpu.make_async_copy(ybuf.at[s], out_hbm.at[pl.ds(0, rows), :], ssem.at[s])

    def issue_gather(idx_ref, s):
        def body(m, carry):
            gather_row(idx_ref, m, s).start()
            return carry
        lax.fori_loop(0, bm, body, 0)

    @pl.when(b == 0)
    def _():
        ybuf[...] = jnp.zeros_like(ybuf)
        dump = pltpu.make_async_copy(
            ybuf.at[0], out_hbm.at[pl.ds(out_hbm.shape[0] - rows, rows), :], ssem.at[0])
        dump.start()
        dump.wait()
        dump = pltpu.make_async_copy(
            ybuf.at[1], out_hbm.at[pl.ds(out_hbm.shape[0] - 2 * rows, rows), :], ssem.at[1])
        dump.start()
        dump.wait()

    @pl.when(jnp.logical_and(b == 0, nused > 0))
    def _():
        issue_gather(src_ref, 0)

    @pl.when(b + 1 < nused)
    def _():
        issue_gather(srcn_ref, 1 - slot)

    @pl.when(b < nused)
    def _():
        gather_all(slot).wait()
        xb = xbuf.at[slot]
        x = jnp.concatenate([xb[pl.ds(c, bm, stride=SUBLANES), :] for c in range(d_model // LANES)],
                            axis=1).astype(BF16)
        gu = jnp.dot(x, wgu_ref[0], preferred_element_type=F32) + bgu_ref[0]
        gate = jnp.minimum(gu[:, :d_ff], SWIGLU_LIMIT)
        up = jnp.clip(gu[:, d_ff:], -SWIGLU_LIMIT, SWIGLU_LIMIT)
        act = gate * jax.nn.sigmoid(gate * SWIGLU_ALPHA) * (up + 1.0)
        y = jnp.dot(act.astype(BF16), wd_ref[0], preferred_element_type=F32) + bd_ref[0]
        gw = jnp.broadcast_to(gate_ref[0], (LANES, bm)).T
        yb = ybuf.at[slot]
        for c in range(d_model // LANES):
            yb[pl.ds(c, bm, stride=SUBLANES), :] = y[:, c * LANES:(c + 1) * LANES] * gw

        def body(m, carry):
            scatter_row(m, slot).start()
            return carry
        lax.fori_loop(0, bm, body, 0)

    @pl.when(jnp.logical_and(b >= 1, b < nused + 1))
    def _():
        scatter_all(1 - slot).wait()

    @pl.when(jnp.logical_and(b == pl.num_programs(0) - 1, b < nused))
    def _():
        scatter_all(slot).wait()


def _combine_kernel(h_ref, y0_ref, y1_ref, y2_ref, y3_ref, out_ref):
    tm = h_ref.shape[0]
    for c in range(h_ref.shape[1] // LANES):
        acc = h_ref[:, c * LANES:(c + 1) * LANES]
        for y_ref in (y0_ref, y1_ref, y2_ref, y3_ref):
            acc = acc + y_ref[pl.ds(c, tm, stride=SUBLANES), :]
        out_ref[:, c * LANES:(c + 1) * LANES] = acc


def _rope_tables(positions):
    inv_freq = ROPE_THETA ** (-jnp.arange(0, ROPE_DIM, 2, dtype=F32) / ROPE_DIM)
    ang = positions.astype(F32).reshape(-1, 1) * inv_freq
    cos, sin = jnp.cos(ang), jnp.sin(ang)
    t = ang.shape[0]
    ones = jnp.ones((t, HEAD_DIM - ROPE_DIM), F32)
    zeros_h = jnp.zeros((t, HEAD_DIM - ROPE_DIM), F32)
    zeros_r = jnp.zeros((t, ROPE_HALF), F32)
    rc = jnp.concatenate([cos, cos, ones], axis=1)
    ra = jnp.concatenate([-sin, zeros_r, zeros_h], axis=1)
    rb = jnp.concatenate([zeros_r, sin, zeros_h], axis=1)
    two = lambda a: jnp.concatenate([a, a], axis=1)
    return two(rc), two(ra), two(rb)


def _dup_heads(w, n_heads):
    d = w.shape[0]
    w = w.reshape(d, n_heads, 1, HEAD_DIM)
    return jnp.broadcast_to(w, (d, n_heads, 2, HEAD_DIM)).reshape(d, n_heads * 2 * HEAD_DIM)


def _full(shape):
    return pl.BlockSpec(shape, lambda *_: (0,) * len(shape))


def kernel(x, positions, norm1_w, w_in, q_norm_w, k_norm_w, sinks, pool_w, pool_scale, w_out, norm2_w,
           w_router, b_router, w_gu, b_gu, w_down, b_down):
    B, S, D = x.shape
    T = B * S
    d_ff = w_down.shape[1]
    tm = ROW_TILE
    assert S % tm == 0 and tm % ATTN_BLOCK == 0 and D % LANES == 0
    n_row_tiles = T // tm
    x2 = x.reshape(T, D)

    q0, k0, v0 = ATTN_WIDTH, ATTN_WIDTH + KV_WIDTH, ATTN_WIDTH + 2 * KV_WIDTH
    w_in_ext = jnp.concatenate([
        w_in[:, :q0], _dup_heads(w_in[:, q0:k0], N_KV_HEADS), _dup_heads(w_in[:, k0:v0], N_KV_HEADS),
        w_in[:, v0:]], axis=1).astype(BF16)
    proj_w = w_in_ext.shape[1]
    scale = HEAD_DIM ** -0.5
    qk_w = jnp.concatenate([jnp.tile(q_norm_w.astype(F32) * scale, N_Q_HEADS),
                            jnp.tile(k_norm_w.astype(F32), 2 * N_KV_HEADS)]).reshape(1, -1)
    head_id = jnp.arange(2 * LANES) // HEAD_DIM
    blockdiag = jnp.where(head_id[:, None] == head_id[None, :], 1.0 / HEAD_DIM, 0.0).astype(BF16)
    rc, ra, rb = _rope_tables(positions)
    row = lambda w: pl.BlockSpec((tm, w), lambda i: (i, 0))
    q_arr, kv_arr, u_arr = pl.pallas_call(
        _in_proj_kernel,
        grid=(n_row_tiles,),
        in_specs=[row(D), _full((1, D)), _full((D, proj_w)), _full((1, qk_w.shape[1])),
                  _full((2 * LANES, 2 * LANES)), row(LANES), row(LANES), row(LANES)],
        out_specs=[row(ATTN_WIDTH), row(4 * KV_WIDTH), row(POOL_WIDTH)],
        out_shape=[jax.ShapeDtypeStruct((T, ATTN_WIDTH), BF16),
                   jax.ShapeDtypeStruct((T, 4 * KV_WIDTH), BF16),
                   jax.ShapeDtypeStruct((T, POOL_WIDTH), F32)],
        compiler_params=pltpu.CompilerParams(dimension_semantics=("parallel",),
                                             vmem_limit_bytes=VMEM_LIMIT_BYTES),
        name="in_proj",
    )(x2, norm1_w.reshape(1, D).astype(F32), w_in_ext, qk_w, blockdiag, rc, ra, rb)

    nq = S // tm
    sub_per_tile = tm // ATTN_BLOCK
    halo_per_tile = tm // POOL_HALO
    rr = jnp.arange(tm)[:, None] + POOL_HALO
    cc = jnp.arange(tm + POOL_HALO)[None, :]
    band = jnp.stack([((cc <= rr) & (cc > rr - w)) for w in POOL_WINDOWS]).astype(BF16)
    tile_spec = lambda w: pl.BlockSpec((tm, w), lambda b, i: (b * nq + i, 0))
    mixed = pl.pallas_call(
        _mixer_kernel,
        grid=(B, nq),
        in_specs=[pl.BlockSpec(memory_space=pltpu.SMEM),
                  tile_spec(ATTN_WIDTH), tile_spec(4 * KV_WIDTH),
                  pl.BlockSpec((ATTN_BLOCK, 4 * KV_WIDTH),
                               lambda b, i: (jnp.maximum((b * nq + i) * sub_per_tile - 1, 0), 0)),
                  tile_spec(POOL_WIDTH),
                  pl.BlockSpec((POOL_HALO, POOL_WIDTH),
                               lambda b, i: (jnp.maximum((b * nq + i) * halo_per_tile - 1, 0), 0)),
                  pl.BlockSpec(band.shape, lambda b, i: (0, 0, 0)),
                  pl.BlockSpec(pool_w.shape, lambda b, i: (0, 0, 0)),
                  pl.BlockSpec((1, POOL_WIDTH), lambda b, i: (0, 0))],
        out_specs=tile_spec(D),
        out_shape=jax.ShapeDtypeStruct((T, D), BF16),
        compiler_params=pltpu.CompilerParams(dimension_semantics=("parallel", "parallel"),
                                             vmem_limit_bytes=VMEM_LIMIT_BYTES),
        name="mixer",
    )(sinks.astype(F32), q_arr, kv_arr, kv_arr, u_arr, u_arr, band, pool_w.astype(BF16),
      pool_scale.reshape(1, POOL_WIDTH).astype(F32))

    wr_pad = jnp.zeros((D, LANES), F32).at[:, :N_EXPERTS].set(w_router.astype(F32))
    wr_hi, wr_lo = _split_bf16(wr_pad)
    br_pad = jnp.zeros((1, LANES), F32).at[0, :N_EXPERTS].set(b_router.astype(F32))
    tri = (jnp.arange(tm)[:, None] < jnp.arange(tm)[None, :]).astype(BF16)
    h_arr, n2_rows, route, gates, cnt = pl.pallas_call(
        _out_router_kernel,
        grid=(n_row_tiles,),
        in_specs=[row(D), row(D), _full((D, D)), _full((1, D)), _full((D, LANES)), _full((D, LANES)),
                  _full((1, LANES)), _full((tm, tm))],
        out_specs=[row(D), pl.BlockSpec((tm * SUBLANES, LANES), lambda i: (i, 0)),
                   pl.BlockSpec((2 * TOP_K, tm), lambda i: (0, i)),
                   pl.BlockSpec((2 * TOP_K, tm), lambda i: (0, i)),
                   _full((N_EXPERTS, LANES))],
        out_shape=[jax.ShapeDtypeStruct((T, D), F32),
                   jax.ShapeDtypeStruct((T * SUBLANES, LANES), F32),
                   jax.ShapeDtypeStruct((2 * TOP_K, T), jnp.int32),
                   jax.ShapeDtypeStruct((2 * TOP_K, T), F32),
                   jax.ShapeDtypeStruct((N_EXPERTS, LANES), F32)],
        scratch_shapes=[pltpu.VMEM((N_EXPERTS, LANES), F32)],
        compiler_params=pltpu.CompilerParams(dimension_semantics=("arbitrary",),
                                             vmem_limit_bytes=VMEM_LIMIT_BYTES),
        name="out_router",
    )(x2, mixed, w_out.astype(BF16), norm2_w.reshape(1, D).astype(F32), wr_hi, wr_lo, br_pad, tri)

    bm = EXPERT_BLOCK
    n_assign = T * TOP_K
    n_blk = -(-n_assign // bm) + N_EXPERTS
    n_slots = n_blk * bm
    counts = cnt[:, 0].astype(jnp.int32)
    padded = ((counts + bm - 1) // bm) * bm
    pend = jnp.cumsum(padded)
    pstart = pend - padded
    expert = route[:TOP_K]
    rank = route[TOP_K:]
    slot = (pstart[expert] + rank).reshape(-1)
    dst_real = (jnp.arange(TOP_K, dtype=jnp.int32)[:, None] * T
                + jnp.arange(T, dtype=jnp.int32)[None, :]).reshape(-1)
    s_all = jnp.arange(n_slots, dtype=jnp.int32)
    dst_pad = n_assign + ((s_all // bm) % 2) * bm + s_all % bm
    slot_dst = dst_pad.at[slot].set(dst_real, unique_indices=True)
    slot_gate = jnp.zeros((n_slots,), F32).at[slot].set(gates[:TOP_K].reshape(-1), unique_indices=True)
    slot_src = jnp.where(slot_dst < n_assign, slot_dst % T, 0)
    blk_start = jnp.arange(n_blk, dtype=jnp.int32) * bm
    blk_expert = jnp.minimum(jnp.sum((blk_start[:, None] >= pend[None, :]).astype(jnp.int32), axis=1),
                             N_EXPERTS - 1)
    n_used = (pend[-1] // bm).astype(jnp.int32).reshape(1)
    slot_src = slot_src.reshape(n_blk, 1, bm)
    slot_dst = slot_dst.reshape(n_blk, 1, bm)
    slot_gate = slot_gate.reshape(n_blk, 1, bm)

    out_rows = (n_assign + 2 * bm) * SUBLANES
    blk_idx = lambda b, be, nu: (b, 0, 0)
    nxt_idx = lambda b, be, nu: (jnp.minimum(b + 1, n_blk - 1), 0, 0)
    exp_idx = lambda b, be, nu: (be[b], 0, 0)
    y_rows = pl.pallas_call(
        _moe_kernel,
        grid_spec=pltpu.PrefetchScalarGridSpec(
            num_scalar_prefetch=2,
            grid=(n_blk,),
            in_specs=[pl.BlockSpec((1, 1, bm), blk_idx, memory_space=pltpu.SMEM),
                      pl.BlockSpec((1, 1, bm), nxt_idx, memory_space=pltpu.SMEM),
                      pl.BlockSpec((1, 1, bm), blk_idx, memory_space=pltpu.SMEM),
                      pl.BlockSpec((1, 1, bm), blk_idx),
                      pl.BlockSpec(memory_space=pl.ANY),
                      pl.BlockSpec((1, D, 2 * d_ff), exp_idx),
                      pl.BlockSpec((1, 1, 2 * d_ff), exp_idx),
                      pl.BlockSpec((1, d_ff, D), exp_idx),
                      pl.BlockSpec((1, 1, D), exp_idx)],
            out_specs=pl.BlockSpec(memory_space=pl.ANY),
            scratch_shapes=[pltpu.VMEM((2, bm * SUBLANES, LANES), F32),
                            pltpu.VMEM((2, bm * SUBLANES, LANES), F32),
                            pltpu.SemaphoreType.DMA((2,)),
                            pltpu.SemaphoreType.DMA((2,))]),
        out_shape=jax.ShapeDtypeStruct((out_rows, LANES), F32),
        compiler_params=pltpu.CompilerParams(dimension_semantics=("arbitrary",),
                                             vmem_limit_bytes=VMEM_LIMIT_BYTES),
        name="moe",
    )(blk_expert, n_used, slot_src, slot_src, slot_dst, slot_gate, n2_rows,
      w_gu.astype(BF16), b_gu.reshape(N_EXPERTS, 1, 2 * d_ff).astype(F32),
      w_down.astype(BF16), b_down.reshape(N_EXPERTS, 1, D).astype(F32))

    yk = lambda k: pl.BlockSpec((tm * SUBLANES, LANES), lambda i: (k * n_row_tiles + i, 0))
    out = pl.pallas_call(
        _combine_kernel,
        grid=(n_row_tiles,),
        in_specs=[row(D), yk(0), yk(1), yk(2), yk(3)],
        out_specs=row(D),
        out_shape=jax.ShapeDtypeStruct((T, D), F32),
        compiler_params=pltpu.CompilerParams(dimension_semantics=("parallel",),
                                             vmem_limit_bytes=VMEM_LIMIT_BYTES),
        name="combine",
    )(h_arr, y_rows, y_rows, y_rows, y_rows)
    return out.reshape(B, S, D)
```

```python
import functools

import jax
import jax.numpy as jnp
from jax import lax
from jax.experimental import pallas as pl
from jax.experimental.pallas import tpu as pltpu

F32 = jnp.float32
BF16 = jnp.bfloat16

HEAD_DIM = 64
N_Q_HEADS = 8
N_KV_HEADS = 2
GQA_GROUP = N_Q_HEADS // N_KV_HEADS
ATTN_WIDTH = N_Q_HEADS * HEAD_DIM
KV_WIDTH = N_KV_HEADS * HEAD_DIM
WINDOW = 128
ATTN_BLOCK = 128
ROPE_THETA = 500000.0
ROPE_DIM = HEAD_DIM // 4
ROPE_HALF = ROPE_DIM // 2
POOL_WINDOWS = (2, 4, 8, 16)
POOL_GROUP_WIDTH = 128
POOL_WIDTH = POOL_GROUP_WIDTH * len(POOL_WINDOWS)
N_EXPERTS = 32
TOP_K = 4
SWIGLU_LIMIT = 7.0
SWIGLU_ALPHA = 1.702
NORM_EPS = 1e-5
QK_NORM_EPS = 1e-6
NEG_INF = -1e30

LANES = 128
SUBLANES = 8
VMEM_LIMIT_BYTES = 56 * 1024 * 1024

ROW_TILE = 512
POOL_HALO = 32
EXPERT_BLOCK = 256
DMA_ISSUE_UNROLL = 16


def _split_bf16(v):
    hi = v.astype(BF16)
    lo = (v - hi.astype(F32)).astype(BF16)
    return hi, lo


def _in_proj_kernel(x_ref, w1_ref, win_ref, qkw_ref, bd_ref, cs_ref, sel_ref, one_ref,
                    q_out, kv_out, u_out):
    x = x_ref[...]
    ms = jnp.mean(x * x, axis=-1, keepdims=True)
    n = (x * lax.rsqrt(ms + NORM_EPS) * w1_ref[...]).astype(BF16)
    proj = jnp.dot(n, win_ref[...], preferred_element_type=F32)
    cs = cs_ref[...]
    cs_hi = cs.astype(BF16)
    r1 = cs - cs_hi.astype(F32)
    cs_mid = r1.astype(BF16)
    cs_lo = (r1 - cs_mid.astype(F32)).astype(BF16)
    sel = sel_ref[...]
    tab = (jnp.dot(cs_hi, sel, preferred_element_type=F32)
           + jnp.dot(cs_mid, sel, preferred_element_type=F32)
           + jnp.dot(cs_lo, sel, preferred_element_type=F32))
    rc = tab[:, 0:LANES] + one_ref[...]
    ra = tab[:, LANES:2 * LANES]
    rb = tab[:, 2 * LANES:3 * LANES]
    bd = bd_ref[...]
    qk_w = qkw_ref[...]
    n_qk_tiles = (ATTN_WIDTH + 2 * KV_WIDTH) // LANES
    for t in range(0, n_qk_tiles, 2):
        blk = proj[:, t * LANES:(t + 2) * LANES]
        hi, lo = _split_bf16(blk * blk)
        msq = (jnp.dot(hi, bd, preferred_element_type=F32)
               + jnp.dot(lo, bd, preferred_element_type=F32))
        normed = blk * lax.rsqrt(msq + QK_NORM_EPS) * qk_w[:, t * LANES:(t + 2) * LANES]
        for s in range(2):
            v = normed[:, s * LANES:(s + 1) * LANES]
            rot = (v * rc + pltpu.roll(v, LANES - ROPE_HALF, 1) * ra
                   + pltpu.roll(v, ROPE_HALF, 1) * rb).astype(BF16)
            col = (t + s) * LANES
            if col < ATTN_WIDTH:
                q_out[:, col:col + LANES] = rot
            else:
                kv_out[:, col - ATTN_WIDTH:col - ATTN_WIDTH + LANES] = rot
    v0 = ATTN_WIDTH + 2 * KV_WIDTH
    kv_out[:, 2 * KV_WIDTH:] = proj[:, v0:v0 + 2 * KV_WIDTH].astype(BF16)
    u_out[...] = proj[:, v0 + 2 * KV_WIDTH:]


def _mixer_kernel(sinks_ref, q_ref, kv_ref, kvh_ref, u_ref, uh_ref, band_ref, pw_ref, ps_ref, out_ref):
    i = pl.program_id(1)
    first = i == 0
    tq = q_ref.shape[0]
    nsub = tq // ATTN_BLOCK
    kw2 = 2 * KV_WIDTH

    lane = lax.broadcasted_iota(jnp.int32, (ATTN_BLOCK, LANES), 1)
    low_half = lane < HEAD_DIM
    qi = lax.broadcasted_iota(jnp.int32, (ATTN_BLOCK, 2 * ATTN_BLOCK), 0)
    kj = lax.broadcasted_iota(jnp.int32, (ATTN_BLOCK, 2 * ATTN_BLOCK), 1)
    band = (kj <= qi + ATTN_BLOCK) & (kj > qi + ATTN_BLOCK - WINDOW)
    band_first = band & (kj >= jnp.where(first, ATTN_BLOCK, 0))

    for j in range(nsub):
        r0 = j * ATTN_BLOCK
        qj = q_ref[r0:r0 + ATTN_BLOCK, :]
        cur = kv_ref[r0:r0 + ATTN_BLOCK, :]
        prev = kvh_ref[...] if j == 0 else kv_ref[r0 - ATTN_BLOCK:r0, :]
        kvcat = jnp.concatenate([prev, cur], axis=0)
        mask1 = band_first if j == 0 else band
        mask = jnp.concatenate([mask1] * GQA_GROUP, axis=0)
        tiles = []
        for g in range(N_KV_HEADS):
            kg = kvcat[:, g * LANES:(g + 1) * LANES]
            vg = kvcat[:, kw2 + g * LANES:kw2 + (g + 1) * LANES]
            pieces = []
            sink_cols = []
            for hh in range(GQA_GROUP):
                h = g * GQA_GROUP + hh
                qt = qj[:, (h // 2) * LANES:(h // 2 + 1) * LANES]
                keep = low_half if h % 2 == 0 else jnp.logical_not(low_half)
                pieces.append(jnp.where(keep, qt, jnp.zeros_like(qt)))
                sink_cols.append(jnp.full((ATTN_BLOCK, 1), sinks_ref[h], F32))
            qs = jnp.concatenate(pieces, axis=0)
            sink = jnp.concatenate(sink_cols, axis=0)
            s = lax.dot_general(qs, kg, (((1,), (1,)), ((), ())), preferred_element_type=F32)
            s = jnp.where(mask, s, NEG_INF)
            m = jnp.maximum(jnp.max(s, axis=-1, keepdims=True), sink)
            p = jnp.exp(s - m)
            denom = jnp.sum(p, axis=-1, keepdims=True) + jnp.exp(sink - m)
            probs = (p / denom).astype(BF16)
            o = jnp.dot(probs, vg, preferred_element_type=F32)
            for pr in range(GQA_GROUP // 2):
                ev = o[(2 * pr) * ATTN_BLOCK:(2 * pr + 1) * ATTN_BLOCK, :]
                od = o[(2 * pr + 1) * ATTN_BLOCK:(2 * pr + 2) * ATTN_BLOCK, :]
                tiles.append(jnp.where(low_half, ev, od))
        out_ref[r0:r0 + ATTN_BLOCK, 0:ATTN_WIDTH] = jnp.concatenate(tiles, axis=1).astype(BF16)

    uc = u_ref[...]
    uh = jnp.where(first, 0.0, uh_ref[...])
    ext = jnp.concatenate([uh, uc], axis=0)
    ext_hi, ext_lo = _split_bf16(ext)
    pos1 = i * tq + lax.broadcasted_iota(jnp.int32, (tq, 1), 0) + 1
    for g, w in enumerate(POOL_WINDOWS):
        c0 = g * POOL_GROUP_WIDTH
        bm = band_ref[g]
        wsum = (jnp.dot(bm, ext_hi[:, c0:c0 + POOL_GROUP_WIDTH], preferred_element_type=F32)
                + jnp.dot(bm, ext_lo[:, c0:c0 + POOL_GROUP_WIDTH], preferred_element_type=F32))
        count = jnp.minimum(pos1, w).astype(F32)
        pooled = wsum / count - uc[:, c0:c0 + POOL_GROUP_WIDTH]
        mixed = jnp.dot(pooled.astype(BF16), pw_ref[g], preferred_element_type=F32)
        mixed = mixed * ps_ref[:, c0:c0 + POOL_GROUP_WIDTH]
        out_ref[:, ATTN_WIDTH + c0:ATTN_WIDTH + c0 + POOL_GROUP_WIDTH] = mixed.astype(BF16)


def _out_router_kernel(x_ref, mix_ref, wout_ref, w2_ref, wrh_ref, wrl_ref, br_ref, tri_ref,
                       h_out, n2_out, route_out, gate_out, cnt_out, base_scr):
    tm = x_ref.shape[0]

    @pl.when(pl.program_id(0) == 0)
    def _():
        base_scr[...] = jnp.zeros_like(base_scr)

    h = x_ref[...] + jnp.dot(mix_ref[...], wout_ref[...], preferred_element_type=F32)
    h_out[...] = h
    ms = jnp.mean(h * h, axis=-1, keepdims=True)
    n2 = h * lax.rsqrt(ms + NORM_EPS) * w2_ref[...]
    for c in range(n2.shape[1] // LANES):
        n2_out[pl.ds(c, tm, stride=SUBLANES), :] = n2[:, c * LANES:(c + 1) * LANES]

    hi, lo = _split_bf16(n2)
    wrh = wrh_ref[...]
    logits = (jnp.dot(hi, wrh, preferred_element_type=F32)
              + jnp.dot(hi, wrl_ref[...], preferred_element_type=F32)
              + jnp.dot(lo, wrh, preferred_element_type=F32)) + br_ref[...]
    lt = logits.T[0:N_EXPERTS, :]
    rows = lax.broadcasted_iota(jnp.int32, (N_EXPERTS, tm), 0)
    picks, vals, idxs = [], [], []
    work = lt
    for _ in range(TOP_K):
        mx = jnp.max(work, axis=0, keepdims=True)
        idx = jnp.min(jnp.where(work == mx, rows, N_EXPERTS), axis=0, keepdims=True)
        pick = rows == idx
        work = jnp.where(pick, -jnp.inf, work)
        picks.append(pick)
        vals.append(mx)
        idxs.append(idx)
    exps = [jnp.exp(v - vals[0]) for v in vals]
    tot = exps[0] + exps[1] + exps[2] + exps[3]
    gates = [e / tot for e in exps]

    onehot = jnp.zeros((N_EXPERTS, tm), F32)
    for pick in picks:
        onehot = onehot + pick.astype(F32)
    before = jnp.dot(onehot.astype(BF16), tri_ref[...], preferred_element_type=F32)
    base = base_scr[...]
    before = before + jnp.concatenate([base] * (tm // LANES), axis=1)
    ranks = [jnp.sum(jnp.where(pick, before, 0.0), axis=0, keepdims=True).astype(jnp.int32)
             for pick in picks]
    route_out[...] = jnp.concatenate(idxs + ranks, axis=0)
    gate_out[...] = jnp.concatenate(gates + gates, axis=0)
    base = base + jnp.broadcast_to(jnp.sum(onehot, axis=1, keepdims=True), base.shape)
    base_scr[...] = base
    cnt_out[...] = base


def _moe_kernel(bexp_ref, nused_ref, src_ref, srcn_ref, dst_ref, n2_hbm,
                wgu_ref, bgu_ref, wd_ref, bd_ref, out_hbm, xbuf, ybuf, gsem, ssem):
    del bexp_ref
    b = pl.program_id(0)
    nused = nused_ref[0]
    bm = src_ref.shape[2]
    rows = bm * SUBLANES
    slot = b % 2
    d_model = wgu_ref.shape[1]
    d_ff = wd_ref.shape[1]

    def gather_row(idx_ref, m, s):
        tok = idx_ref[0, 0, m]
        return pltpu.make_async_copy(
            n2_hbm.at[pl.ds(pl.multiple_of(tok * SUBLANES, SUBLANES), SUBLANES), :],
            xbuf.at[s, pl.ds(pl.multiple_of(m * SUBLANES, SUBLANES), SUBLANES), :],
            gsem.at[s])

    def scatter_row(m, s):
        dst = dst_ref[0, 0, m]
        return pltpu.make_async_copy(
            ybuf.at[s, pl.ds(pl.multiple_of(m * SUBLANES, SUBLANES), SUBLANES), :],
            out_hbm.at[pl.ds(pl.multiple_of(dst * SUBLANES, SUBLANES), SUBLANES), :],
            ssem.at[s])

    def gather_all(s):
        return pltpu.make_async_copy(n2_hbm.at[pl.ds(0, rows), :], xbuf.at[s], gsem.at[s])

    def scatter_all(s):
        return pltpu.make_async_copy(ybuf.at[s], out_hbm.at[pl.ds(0, rows), :], ssem.at[s])

    def issue_rows(start_row):
        def body(g, carry):
            for r in range(DMA_ISSUE_UNROLL):
                start_row(g * DMA_ISSUE_UNROLL + r)
            return carry
        lax.fori_loop(0, bm // DMA_ISSUE_UNROLL, body, 0)

    def issue_gather(idx_ref, s):
        issue_rows(lambda m: gather_row(idx_ref, m, s).start())

    @pl.when(b == 0)
    def _():
        ybuf[...] = jnp.zeros_like(ybuf)
        dump = pltpu.make_async_copy(
            ybuf.at[0], out_hbm.at[pl.ds(out_hbm.shape[0] - rows, rows), :], ssem.at[0])
        dump.start()
        dump.wait()
        dump = pltpu.make_async_copy(
            ybuf.at[1], out_hbm.at[pl.ds(out_hbm.shape[0] - 2 * rows, rows), :], ssem.at[1])
        dump.start()
        dump.wait()

    @pl.when(jnp.logical_and(b == 0, nused > 0))
    def _():
        issue_gather(src_ref, 0)

    @pl.when(b + 1 < nused)
    def _():
        issue_gather(srcn_ref, 1 - slot)

    @pl.when(b < nused)
    def _():
        gather_all(slot).wait()
        xb = xbuf.at[slot]
        x = jnp.concatenate([xb[pl.ds(c, bm, stride=SUBLANES), :] for c in range(d_model // LANES)],
                            axis=1).astype(BF16)
        gu = jnp.dot(x, wgu_ref[0], preferred_element_type=F32) + bgu_ref[0]
        gate = jnp.minimum(gu[:, :d_ff], SWIGLU_LIMIT)
        up = jnp.clip(gu[:, d_ff:], -SWIGLU_LIMIT, SWIGLU_LIMIT)
        act = gate * jax.nn.sigmoid(gate * SWIGLU_ALPHA) * (up + 1.0)
        y = jnp.dot(act.astype(BF16), wd_ref[0], preferred_element_type=F32) + bd_ref[0]
        yb = ybuf.at[slot]
        for c in range(d_model // LANES):
            yb[pl.ds(c, bm, stride=SUBLANES), :] = y[:, c * LANES:(c + 1) * LANES]
        issue_rows(lambda m: scatter_row(m, slot).start())

    @pl.when(jnp.logical_and(b >= 1, b < nused + 1))
    def _():
        scatter_all(1 - slot).wait()

    @pl.when(jnp.logical_and(b == pl.num_programs(0) - 1, b < nused))
    def _():
        scatter_all(slot).wait()


def _combine_kernel(h_ref, g_ref, y0_ref, y1_ref, y2_ref, y3_ref, out_ref):
    tm = h_ref.shape[0]
    y_refs = (y0_ref, y1_ref, y2_ref, y3_ref)
    gw = [jnp.broadcast_to(g_ref[k:k + 1, :], (LANES, tm)).T for k in range(TOP_K)]
    for c in range(h_ref.shape[1] // LANES):
        acc = h_ref[:, c * LANES:(c + 1) * LANES]
        for k in range(TOP_K):
            acc = acc + y_refs[k][pl.ds(c, tm, stride=SUBLANES), :] * gw[k]
        out_ref[:, c * LANES:(c + 1) * LANES] = acc


def _rope_tables(positions):
    inv_freq = ROPE_THETA ** (-jnp.arange(0, ROPE_DIM, 2, dtype=F32) / ROPE_DIM)
    ang = positions.astype(F32).reshape(-1, 1) * inv_freq
    cs = jnp.concatenate([jnp.cos(ang), jnp.sin(ang)], axis=1)
    d = jnp.arange(LANES) % HEAD_DIM
    r = jnp.arange(2 * ROPE_HALF)[:, None]
    sel_c = ((r < ROPE_HALF) & (d[None, :] < ROPE_DIM) & (d[None, :] % ROPE_HALF == r)).astype(F32)
    sel_a = -((r >= ROPE_HALF) & (d[None, :] < ROPE_HALF) & (d[None, :] == r - ROPE_HALF)).astype(F32)
    sel_b = ((r >= ROPE_HALF) & (d[None, :] >= ROPE_HALF) & (d[None, :] < ROPE_DIM)
             & (d[None, :] - ROPE_HALF == r - ROPE_HALF)).astype(F32)
    sel = jnp.concatenate([sel_c, sel_a, sel_b], axis=1).astype(BF16)
    ones = (d >= ROPE_DIM).astype(F32).reshape(1, LANES)
    return cs, sel, ones


def _dup_heads(w, n_heads):
    d = w.shape[0]
    w = w.reshape(d, n_heads, 1, HEAD_DIM)
    return jnp.broadcast_to(w, (d, n_heads, 2, HEAD_DIM)).reshape(d, n_heads * 2 * HEAD_DIM)


def _full(shape):
    return pl.BlockSpec(shape, lambda *_: (0,) * len(shape))


def kernel(x, positions, norm1_w, w_in, q_norm_w, k_norm_w, sinks, pool_w, pool_scale, w_out, norm2_w,
           w_router, b_router, w_gu, b_gu, w_down, b_down):
    B, S, D = x.shape
    T = B * S
    d_ff = w_down.shape[1]
    tm = ROW_TILE
    assert S % tm == 0 and tm % ATTN_BLOCK == 0 and D % LANES == 0
    n_row_tiles = T // tm
    x2 = x.reshape(T, D)

    q0, k0, v0 = ATTN_WIDTH, ATTN_WIDTH + KV_WIDTH, ATTN_WIDTH + 2 * KV_WIDTH
    w_in_ext = jnp.concatenate([
        w_in[:, :q0], _dup_heads(w_in[:, q0:k0], N_KV_HEADS), _dup_heads(w_in[:, k0:v0], N_KV_HEADS),
        w_in[:, v0:]], axis=1).astype(BF16)
    proj_w = w_in_ext.shape[1]
    scale = HEAD_DIM ** -0.5
    qk_w = jnp.concatenate([jnp.tile(q_norm_w.astype(F32) * scale, N_Q_HEADS),
                            jnp.tile(k_norm_w.astype(F32), 2 * N_KV_HEADS)]).reshape(1, -1)
    head_id = jnp.arange(2 * LANES) // HEAD_DIM
    blockdiag = jnp.where(head_id[:, None] == head_id[None, :], 1.0 / HEAD_DIM, 0.0).astype(BF16)
    cs, sel, ones_mask = _rope_tables(positions)
    row = lambda w: pl.BlockSpec((tm, w), lambda i: (i, 0))
    q_arr, kv_arr, u_arr = pl.pallas_call(
        _in_proj_kernel,
        grid=(n_row_tiles,),
        in_specs=[row(D), _full((1, D)), _full((D, proj_w)), _full((1, qk_w.shape[1])),
                  _full((2 * LANES, 2 * LANES)), row(2 * ROPE_HALF), _full(sel.shape), _full((1, LANES))],
        out_specs=[row(ATTN_WIDTH), row(4 * KV_WIDTH), row(POOL_WIDTH)],
        out_shape=[jax.ShapeDtypeStruct((T, ATTN_WIDTH), BF16),
                   jax.ShapeDtypeStruct((T, 4 * KV_WIDTH), BF16),
                   jax.ShapeDtypeStruct((T, POOL_WIDTH), F32)],
        compiler_params=pltpu.CompilerParams(dimension_semantics=("parallel",),
                                             vmem_limit_bytes=VMEM_LIMIT_BYTES),
        name="in_proj",
    )(x2, norm1_w.reshape(1, D).astype(F32), w_in_ext, qk_w, blockdiag, cs, sel, ones_mask)

    nq = S // tm
    sub_per_tile = tm // ATTN_BLOCK
    halo_per_tile = tm // POOL_HALO
    rr = jnp.arange(tm)[:, None] + POOL_HALO
    cc = jnp.arange(tm + POOL_HALO)[None, :]
    band = jnp.stack([((cc <= rr) & (cc > rr - w)) for w in POOL_WINDOWS]).astype(BF16)
    tile_spec = lambda w: pl.BlockSpec((tm, w), lambda b, i: (b * nq + i, 0))
    mixed = pl.pallas_call(
        _mixer_kernel,
        grid=(B, nq),
        in_specs=[pl.BlockSpec(memory_space=pltpu.SMEM),
                  tile_spec(ATTN_WIDTH), tile_spec(4 * KV_WIDTH),
                  pl.BlockSpec((ATTN_BLOCK, 4 * KV_WIDTH),
                               lambda b, i: (jnp.maximum((b * nq + i) * sub_per_tile - 1, 0), 0)),
                  tile_spec(POOL_WIDTH),
                  pl.BlockSpec((POOL_HALO, POOL_WIDTH),
                               lambda b, i: (jnp.maximum((b * nq + i) * halo_per_tile - 1, 0), 0)),
                  pl.BlockSpec(band.shape, lambda b, i: (0, 0, 0)),
                  pl.BlockSpec(pool_w.shape, lambda b, i: (0, 0, 0)),
                  pl.BlockSpec((1, POOL_WIDTH), lambda b, i: (0, 0))],
        out_specs=tile_spec(D),
        out_shape=jax.ShapeDtypeStruct((T, D), BF16),
        compiler_params=pltpu.CompilerParams(dimension_semantics=("parallel", "parallel"),
                                             vmem_limit_bytes=VMEM_LIMIT_BYTES),
        name="mixer",
    )(sinks.astype(F32), q_arr, kv_arr, kv_arr, u_arr, u_arr, band, pool_w.astype(BF16),
      pool_scale.reshape(1, POOL_WIDTH).astype(F32))

    wr_pad = jnp.zeros((D, LANES), F32).at[:, :N_EXPERTS].set(w_router.astype(F32))
    wr_hi, wr_lo = _split_bf16(wr_pad)
    br_pad = jnp.zeros((1, LANES), F32).at[0, :N_EXPERTS].set(b_router.astype(F32))
    tri = (jnp.arange(tm)[:, None] < jnp.arange(tm)[None, :]).astype(BF16)
    h_arr, n2_rows, route, gates, cnt = pl.pallas_call(
        _out_router_kernel,
        grid=(n_row_tiles,),
        in_specs=[row(D), row(D), _full((D, D)), _full((1, D)), _full((D, LANES)), _full((D, LANES)),
                  _full((1, LANES)), _full((tm, tm))],
        out_specs=[row(D), pl.BlockSpec((tm * SUBLANES, LANES), lambda i: (i, 0)),
                   pl.BlockSpec((2 * TOP_K, tm), lambda i: (0, i)),
                   pl.BlockSpec((2 * TOP_K, tm), lambda i: (0, i)),
                   _full((N_EXPERTS, LANES))],
        out_shape=[jax.ShapeDtypeStruct((T, D), F32),
                   jax.ShapeDtypeStruct((T * SUBLANES, LANES), F32),
                   jax.ShapeDtypeStruct((2 * TOP_K, T), jnp.int32),
                   jax.ShapeDtypeStruct((2 * TOP_K, T), F32),
                   jax.ShapeDtypeStruct((N_EXPERTS, LANES), F32)],
        scratch_shapes=[pltpu.VMEM((N_EXPERTS, LANES), F32)],
        compiler_params=pltpu.CompilerParams(dimension_semantics=("arbitrary",),
                                             vmem_limit_bytes=VMEM_LIMIT_BYTES),
        name="out_router",
    )(x2, mixed, w_out.astype(BF16), norm2_w.reshape(1, D).astype(F32), wr_hi, wr_lo, br_pad, tri)

    bm = EXPERT_BLOCK
    n_assign = T * TOP_K
    n_blk = -(-n_assign // bm) + N_EXPERTS
    n_slots = n_blk * bm
    counts = cnt[:, 0].astype(jnp.int32)
    padded = ((counts + bm - 1) // bm) * bm
    pend = jnp.cumsum(padded)
    pstart = pend - padded
    expert = route[:TOP_K]
    rank = route[TOP_K:]
    start_of = jnp.zeros_like(rank)
    for e in range(N_EXPERTS):
        start_of = jnp.where(expert == e, pstart[e], start_of)
    slot = (start_of + rank).reshape(-1)
    dst_real = (jnp.arange(TOP_K, dtype=jnp.int32)[:, None] * T
                + jnp.arange(T, dtype=jnp.int32)[None, :]).reshape(-1)
    s_all = jnp.arange(n_slots, dtype=jnp.int32)
    dst_pad = n_assign + ((s_all // bm) % 2) * bm + s_all % bm
    slot_dst = dst_pad.at[slot].set(dst_real, unique_indices=True)
    slot_src = jnp.where(slot_dst < n_assign, slot_dst % T, 0)
    blk_start = jnp.arange(n_blk, dtype=jnp.int32) * bm
    blk_expert = jnp.minimum(jnp.sum((blk_start[:, None] >= pend[None, :]).astype(jnp.int32), axis=1),
                             N_EXPERTS - 1)
    n_used = (pend[-1] // bm).astype(jnp.int32).reshape(1)
    slot_src = slot_src.reshape(n_blk, 1, bm)
    slot_dst = slot_dst.reshape(n_blk, 1, bm)

    out_rows = (n_assign + 2 * bm) * SUBLANES
    blk_idx = lambda b, be, nu: (b, 0, 0)
    nxt_idx = lambda b, be, nu: (jnp.minimum(b + 1, n_blk - 1), 0, 0)
    exp_idx = lambda b, be, nu: (be[b], 0, 0)
    y_rows = pl.pallas_call(
        _moe_kernel,
        grid_spec=pltpu.PrefetchScalarGridSpec(
            num_scalar_prefetch=2,
            grid=(n_blk,),
            in_specs=[pl.BlockSpec((1, 1, bm), blk_idx, memory_space=pltpu.SMEM),
                      pl.BlockSpec((1, 1, bm), nxt_idx, memory_space=pltpu.SMEM),
                      pl.BlockSpec((1, 1, bm), blk_idx, memory_space=pltpu.SMEM),
                      pl.BlockSpec(memory_space=pl.ANY),
                      pl.BlockSpec((1, D, 2 * d_ff), exp_idx),
                      pl.BlockSpec((1, 1, 2 * d_ff), exp_idx),
                      pl.BlockSpec((1, d_ff, D), exp_idx),
                      pl.BlockSpec((1, 1, D), exp_idx)],
            out_specs=pl.BlockSpec(memory_space=pl.ANY),
            scratch_shapes=[pltpu.VMEM((2, bm * SUBLANES, LANES), F32),
                            pltpu.VMEM((2, bm * SUBLANES, LANES), F32),
                            pltpu.SemaphoreType.DMA((2,)),
                            pltpu.SemaphoreType.DMA((2,))]),
        out_shape=jax.ShapeDtypeStruct((out_rows, LANES), F32),
        compiler_params=pltpu.CompilerParams(dimension_semantics=("arbitrary",),
                                             vmem_limit_bytes=VMEM_LIMIT_BYTES),
        name="moe",
    )(blk_expert, n_used, slot_src, slot_src, slot_dst, n2_rows,
      w_gu.astype(BF16), b_gu.reshape(N_EXPERTS, 1, 2 * d_ff).astype(F32),
      w_down.astype(BF16), b_down.reshape(N_EXPERTS, 1, D).astype(F32))

    yk = lambda k: pl.BlockSpec((tm * SUBLANES, LANES), lambda i: (k * n_row_tiles + i, 0))
    out = pl.pallas_call(
        _combine_kernel,
        grid=(n_row_tiles,),
        in_specs=[row(D), pl.BlockSpec((2 * TOP_K, tm), lambda i: (0, i)), yk(0), yk(1), yk(2), yk(3)],
        out_specs=row(D),
        out_shape=jax.ShapeDtypeStruct((T, D), F32),
        compiler_params=pltpu.CompilerParams(dimension_semantics=("parallel",),
                                             vmem_limit_bytes=VMEM_LIMIT_BYTES),
        name="combine",
    )(h_arr, gates, y_rows, y_rows, y_rows, y_rows)
    return out.reshape(B, S, D)
```

```python
import functools

import jax
import jax.numpy as jnp
from jax import lax
from jax.experimental import pallas as pl
from jax.experimental.pallas import tpu as pltpu

F32 = jnp.float32
BF16 = jnp.bfloat16

HEAD_DIM = 64
N_Q_HEADS = 8
N_KV_HEADS = 2
GQA_GROUP = N_Q_HEADS // N_KV_HEADS
ATTN_WIDTH = N_Q_HEADS * HEAD_DIM
KV_WIDTH = N_KV_HEADS * HEAD_DIM
WINDOW = 128
ATTN_BLOCK = 128
ROPE_THETA = 500000.0
ROPE_DIM = HEAD_DIM // 4
ROPE_HALF = ROPE_DIM // 2
POOL_WINDOWS = (2, 4, 8, 16)
POOL_GROUP_WIDTH = 128
POOL_WIDTH = POOL_GROUP_WIDTH * len(POOL_WINDOWS)
N_EXPERTS = 32
TOP_K = 4
SWIGLU_LIMIT = 7.0
SWIGLU_ALPHA = 1.702
NORM_EPS = 1e-5
QK_NORM_EPS = 1e-6
NEG_INF = -1e30

LANES = 128
SUBLANES = 8
VMEM_LIMIT_BYTES = 56 * 1024 * 1024

ROW_TILE = 512
POOL_HALO = 32
POOL_LEAD = 16
EXPERT_BLOCK = 256
DMA_ISSUE_UNROLL = 16


def _split_bf16(v):
    hi = v.astype(BF16)
    lo = (v - hi.astype(F32)).astype(BF16)
    return hi, lo


def _in_proj_kernel(x_ref, w1_ref, win_ref, qkw_ref, bd_ref, cs_ref, sel_ref, one_ref,
                    q_out, kv_out, u_out):
    x = x_ref[...]
    ms = jnp.mean(x * x, axis=-1, keepdims=True)
    n = (x * lax.rsqrt(ms + NORM_EPS) * w1_ref[...]).astype(BF16)
    proj = jnp.dot(n, win_ref[...], preferred_element_type=F32)
    cs = cs_ref[...]
    cs_hi = cs.astype(BF16)
    r1 = cs - cs_hi.astype(F32)
    cs_mid = r1.astype(BF16)
    cs_lo = (r1 - cs_mid.astype(F32)).astype(BF16)
    sel = sel_ref[...]
    tab = (jnp.dot(cs_hi, sel, preferred_element_type=F32)
           + jnp.dot(cs_mid, sel, preferred_element_type=F32)
           + jnp.dot(cs_lo, sel, preferred_element_type=F32))
    rc = tab[:, 0:LANES] + one_ref[...]
    ra = tab[:, LANES:2 * LANES]
    rb = tab[:, 2 * LANES:3 * LANES]
    bd = bd_ref[...]
    qk_w = qkw_ref[...]
    n_qk_tiles = (ATTN_WIDTH + 2 * KV_WIDTH) // LANES
    for t in range(0, n_qk_tiles, 2):
        blk = proj[:, t * LANES:(t + 2) * LANES]
        hi, lo = _split_bf16(blk * blk)
        msq = (jnp.dot(hi, bd, preferred_element_type=F32)
               + jnp.dot(lo, bd, preferred_element_type=F32))
        normed = blk * lax.rsqrt(msq + QK_NORM_EPS) * qk_w[:, t * LANES:(t + 2) * LANES]
        for s in range(2):
            v = normed[:, s * LANES:(s + 1) * LANES]
            rot = (v * rc + pltpu.roll(v, LANES - ROPE_HALF, 1) * ra
                   + pltpu.roll(v, ROPE_HALF, 1) * rb).astype(BF16)
            col = (t + s) * LANES
            if col < ATTN_WIDTH:
                q_out[:, col:col + LANES] = rot
            else:
                kv_out[:, col - ATTN_WIDTH:col - ATTN_WIDTH + LANES] = rot
    v0 = ATTN_WIDTH + 2 * KV_WIDTH
    kv_out[:, 2 * KV_WIDTH:] = proj[:, v0:v0 + 2 * KV_WIDTH].astype(BF16)
    u_out[...] = proj[:, v0 + 2 * KV_WIDTH:]


def _mixer_kernel(sinks_ref, q_ref, kv_ref, kvh_ref, u_ref, uh_ref, band_ref, pw_ref, ps_ref, out_ref):
    i = pl.program_id(1)
    first = i == 0
    tq = q_ref.shape[0]
    nsub = tq // ATTN_BLOCK
    kw2 = 2 * KV_WIDTH

    lane = lax.broadcasted_iota(jnp.int32, (ATTN_BLOCK, LANES), 1)
    low_half = lane < HEAD_DIM
    qi = lax.broadcasted_iota(jnp.int32, (ATTN_BLOCK, 2 * ATTN_BLOCK), 0)
    kj = lax.broadcasted_iota(jnp.int32, (ATTN_BLOCK, 2 * ATTN_BLOCK), 1)
    band = (kj <= qi + ATTN_BLOCK) & (kj > qi + ATTN_BLOCK - WINDOW)
    band_first = band & (kj >= jnp.where(first, ATTN_BLOCK, 0))

    for j in range(nsub):
        r0 = j * ATTN_BLOCK
        qj = q_ref[r0:r0 + ATTN_BLOCK, :]
        cur = kv_ref[r0:r0 + ATTN_BLOCK, :]
        prev = kvh_ref[...] if j == 0 else kv_ref[r0 - ATTN_BLOCK:r0, :]
        kvcat = jnp.concatenate([prev, cur], axis=0)
        mask = band_first if j == 0 else band
        tiles = []
        for pair in range(N_Q_HEADS // 2):
            g = (2 * pair) // GQA_GROUP
            kg = kvcat[:, g * LANES:(g + 1) * LANES]
            vg = kvcat[:, kw2 + g * LANES:kw2 + (g + 1) * LANES]
            qt = qj[:, pair * LANES:(pair + 1) * LANES]
            halves = []
            for half in range(2):
                sink = sinks_ref[2 * pair + half]
                keep = low_half if half == 0 else jnp.logical_not(low_half)
                qh = jnp.where(keep, qt, jnp.zeros_like(qt))
                s = lax.dot_general(qh, kg, (((1,), (1,)), ((), ())), preferred_element_type=F32)
                s = jnp.where(mask, s, NEG_INF)
                m = jnp.maximum(jnp.max(s, axis=-1, keepdims=True), sink)
                p = jnp.exp(s - m)
                denom = jnp.sum(p, axis=-1, keepdims=True) + jnp.exp(sink - m)
                o = jnp.dot(p.astype(BF16), vg, preferred_element_type=F32)
                halves.append(o / denom)
            tiles.append(jnp.where(low_half, halves[0], halves[1]))
        out_ref[r0:r0 + ATTN_BLOCK, 0:ATTN_WIDTH] = jnp.concatenate(tiles, axis=1).astype(BF16)

    uc = u_ref[...]
    uh = jnp.where(first, 0.0, uh_ref[...])
    ext = jnp.concatenate([uh, uc], axis=0)
    ext_hi, ext_lo = _split_bf16(ext)
    pos1 = i * tq + lax.broadcasted_iota(jnp.int32, (tq, 1), 0) + 1
    span = band_ref.shape[2]
    lead = span - ATTN_BLOCK
    for g, w in enumerate(POOL_WINDOWS):
        c0 = g * POOL_GROUP_WIDTH
        bm = band_ref[g]
        parts = []
        for j in range(nsub):
            e0 = j * ATTN_BLOCK + POOL_HALO - lead
            rhs = jnp.concatenate([ext_hi[e0:e0 + span, c0:c0 + POOL_GROUP_WIDTH],
                                   ext_lo[e0:e0 + span, c0:c0 + POOL_GROUP_WIDTH]], axis=1)
            both = jnp.dot(bm, rhs, preferred_element_type=F32)
            parts.append(both[:, :POOL_GROUP_WIDTH] + both[:, POOL_GROUP_WIDTH:])
        wsum = jnp.concatenate(parts, axis=0)
        count = jnp.minimum(pos1, w).astype(F32)
        pooled = wsum / count - uc[:, c0:c0 + POOL_GROUP_WIDTH]
        mixed = jnp.dot(pooled.astype(BF16), pw_ref[g], preferred_element_type=F32)
        mixed = mixed * ps_ref[:, c0:c0 + POOL_GROUP_WIDTH]
        out_ref[:, ATTN_WIDTH + c0:ATTN_WIDTH + c0 + POOL_GROUP_WIDTH] = mixed.astype(BF16)


def _out_router_kernel(x_ref, mix_ref, wout_ref, w2_ref, wrh_ref, wrl_ref, br_ref, tri_ref,
                       h_out, n2_out, route_out, gate_out, cnt_out, base_scr):
    tm = x_ref.shape[0]

    @pl.when(pl.program_id(0) == 0)
    def _():
        base_scr[...] = jnp.zeros_like(base_scr)

    h = x_ref[...] + jnp.dot(mix_ref[...], wout_ref[...], preferred_element_type=F32)
    h_out[...] = h
    ms = jnp.mean(h * h, axis=-1, keepdims=True)
    n2 = h * lax.rsqrt(ms + NORM_EPS) * w2_ref[...]
    for c in range(n2.shape[1] // LANES):
        n2_out[pl.ds(c, tm, stride=SUBLANES), :] = n2[:, c * LANES:(c + 1) * LANES]

    hi, lo = _split_bf16(n2)
    wrh = wrh_ref[...]
    logits = (jnp.dot(hi, wrh, preferred_element_type=F32)
              + jnp.dot(hi, wrl_ref[...], preferred_element_type=F32)
              + jnp.dot(lo, wrh, preferred_element_type=F32)) + br_ref[...]
    lt = logits.T[0:N_EXPERTS, :]
    rows = lax.broadcasted_iota(jnp.int32, (N_EXPERTS, tm), 0)
    picks, vals, idxs = [], [], []
    work = lt
    for _ in range(TOP_K):
        mx = jnp.max(work, axis=0, keepdims=True)
        idx = jnp.min(jnp.where(work == mx, rows, N_EXPERTS), axis=0, keepdims=True)
        pick = rows == idx
        work = jnp.where(pick, -jnp.inf, work)
        picks.append(pick)
        vals.append(mx)
        idxs.append(idx)
    exps = [jnp.exp(v - vals[0]) for v in vals]
    tot = exps[0] + exps[1] + exps[2] + exps[3]
    gates = [e / tot for e in exps]

    onehot = jnp.zeros((N_EXPERTS, tm), F32)
    for pick in picks:
        onehot = onehot + pick.astype(F32)
    before = jnp.dot(onehot.astype(BF16), tri_ref[...], preferred_element_type=F32)
    base = base_scr[...]
    before = before + jnp.concatenate([base] * (tm // LANES), axis=1)
    ranks = [jnp.sum(jnp.where(pick, before, 0.0), axis=0, keepdims=True).astype(jnp.int32)
             for pick in picks]
    route_out[...] = jnp.concatenate(idxs + ranks, axis=0)
    gate_out[...] = jnp.concatenate(gates + gates, axis=0)
    base = base + jnp.broadcast_to(jnp.sum(onehot, axis=1, keepdims=True), base.shape)
    base_scr[...] = base
    cnt_out[...] = base


def _moe_kernel(bexp_ref, nused_ref, src_ref, srcn_ref, dst_ref, n2_hbm,
                wgu_ref, bgu_ref, wd_ref, bd_ref, out_hbm, xbuf0, xbuf1, ybuf0, ybuf1, gsem, ssem):
    del bexp_ref
    b = pl.program_id(0)
    nused = nused_ref[0]
    bm = src_ref.shape[2]
    rows = bm * SUBLANES
    d_model = wgu_ref.shape[1]
    d_ff = wd_ref.shape[1]
    xbufs = (xbuf0, xbuf1)
    ybufs = (ybuf0, ybuf1)

    def row_slice(m):
        r0 = m * SUBLANES
        return pl.ds(r0 if isinstance(m, int) else pl.multiple_of(r0, SUBLANES), SUBLANES)

    def gather_row(idx_ref, m, s):
        tok = idx_ref[0, 0, m]
        return pltpu.make_async_copy(
            n2_hbm.at[pl.ds(pl.multiple_of(tok * SUBLANES, SUBLANES), SUBLANES), :],
            xbufs[s].at[row_slice(m), :], gsem.at[s])

    def scatter_row(m, s):
        dst = dst_ref[0, 0, m]
        return pltpu.make_async_copy(
            ybufs[s].at[row_slice(m), :],
            out_hbm.at[pl.ds(pl.multiple_of(dst * SUBLANES, SUBLANES), SUBLANES), :], ssem.at[s])

    def gather_all(s):
        return pltpu.make_async_copy(n2_hbm.at[pl.ds(0, rows), :], xbufs[s], gsem.at[s])

    def scatter_all(s):
        return pltpu.make_async_copy(ybufs[s], out_hbm.at[pl.ds(0, rows), :], ssem.at[s])

    @pl.when(b == 0)
    def _():
        for s in range(2):
            ybufs[s][...] = jnp.zeros_like(ybufs[s])
            dump = pltpu.make_async_copy(
                ybufs[s], out_hbm.at[pl.ds(out_hbm.shape[0] - (s + 1) * rows, rows), :], ssem.at[s])
            dump.start()
            dump.wait()

    @pl.when(jnp.logical_and(b == 0, nused > 0))
    def _():
        def body(g, carry):
            for r in range(DMA_ISSUE_UNROLL):
                gather_row(src_ref, g * DMA_ISSUE_UNROLL + r, 0).start()
            return carry
        lax.fori_loop(0, bm // DMA_ISSUE_UNROLL, body, 0)

    def step(s):
        @pl.when(b + 1 < nused)
        def _():
            for m in range(bm):
                gather_row(srcn_ref, m, 1 - s).start()

        @pl.when(b < nused)
        def _():
            gather_all(s).wait()
            xb = xbufs[s]
            x = jnp.concatenate([xb[pl.ds(c, bm, stride=SUBLANES), :] for c in range(d_model // LANES)],
                                axis=1).astype(BF16)
            gu = jnp.dot(x, wgu_ref[0], preferred_element_type=F32) + bgu_ref[0]
            gate = jnp.minimum(gu[:, :d_ff], SWIGLU_LIMIT)
            up = jnp.clip(gu[:, d_ff:], -SWIGLU_LIMIT, SWIGLU_LIMIT)
            act = gate * jax.nn.sigmoid(gate * SWIGLU_ALPHA) * (up + 1.0)
            y = jnp.dot(act.astype(BF16), wd_ref[0], preferred_element_type=F32) + bd_ref[0]
            yb = ybufs[s]
            for c in range(d_model // LANES):
                yb[pl.ds(c, bm, stride=SUBLANES), :] = y[:, c * LANES:(c + 1) * LANES]
            for m in range(bm):
                scatter_row(m, s).start()

        @pl.when(jnp.logical_and(b >= 1, b < nused + 1))
        def _():
            scatter_all(1 - s).wait()

        @pl.when(jnp.logical_and(b == pl.num_programs(0) - 1, b < nused))
        def _():
            scatter_all(s).wait()

    for s in range(2):
        pl.when(b % 2 == s)(functools.partial(step, s))


def _combine_kernel(h_ref, g_ref, y0_ref, y1_ref, y2_ref, y3_ref, out_ref):
    tm = h_ref.shape[0]
    y_refs = (y0_ref, y1_ref, y2_ref, y3_ref)
    gw = [jnp.broadcast_to(g_ref[k:k + 1, :], (LANES, tm)).T for k in range(TOP_K)]
    for c in range(h_ref.shape[1] // LANES):
        acc = h_ref[:, c * LANES:(c + 1) * LANES]
        for k in range(TOP_K):
            acc = acc + y_refs[k][pl.ds(c, tm, stride=SUBLANES), :] * gw[k]
        out_ref[:, c * LANES:(c + 1) * LANES] = acc


def _rope_tables(positions):
    inv_freq = ROPE_THETA ** (-jnp.arange(0, ROPE_DIM, 2, dtype=F32) / ROPE_DIM)
    ang = positions.astype(F32).reshape(-1, 1) * inv_freq
    cs = jnp.concatenate([jnp.cos(ang), jnp.sin(ang)], axis=1)
    d = jnp.arange(LANES) % HEAD_DIM
    r = jnp.arange(2 * ROPE_HALF)[:, None]
    sel_c = ((r < ROPE_HALF) & (d[None, :] < ROPE_DIM) & (d[None, :] % ROPE_HALF == r)).astype(F32)
    sel_a = -((r >= ROPE_HALF) & (d[None, :] < ROPE_HALF) & (d[None, :] == r - ROPE_HALF)).astype(F32)
    sel_b = ((r >= ROPE_HALF) & (d[None, :] >= ROPE_HALF) & (d[None, :] < ROPE_DIM)
             & (d[None, :] - ROPE_HALF == r - ROPE_HALF)).astype(F32)
    sel = jnp.concatenate([sel_c, sel_a, sel_b], axis=1).astype(BF16)
    ones = (d >= ROPE_DIM).astype(F32).reshape(1, LANES)
    return cs, sel, ones


def _dup_heads(w, n_heads):
    d = w.shape[0]
    w = w.reshape(d, n_heads, 1, HEAD_DIM)
    return jnp.broadcast_to(w, (d, n_heads, 2, HEAD_DIM)).reshape(d, n_heads * 2 * HEAD_DIM)


def _full(shape):
    return pl.BlockSpec(shape, lambda *_: (0,) * len(shape))


def kernel(x, positions, norm1_w, w_in, q_norm_w, k_norm_w, sinks, pool_w, pool_scale, w_out, norm2_w,
           w_router, b_router, w_gu, b_gu, w_down, b_down):
    B, S, D = x.shape
    T = B * S
    d_ff = w_down.shape[1]
    tm = ROW_TILE
    assert S % tm == 0 and tm % ATTN_BLOCK == 0 and D % LANES == 0
    n_row_tiles = T // tm
    x2 = x.reshape(T, D)

    q0, k0, v0 = ATTN_WIDTH, ATTN_WIDTH + KV_WIDTH, ATTN_WIDTH + 2 * KV_WIDTH
    w_in_ext = jnp.concatenate([
        w_in[:, :q0], _dup_heads(w_in[:, q0:k0], N_KV_HEADS), _dup_heads(w_in[:, k0:v0], N_KV_HEADS),
        w_in[:, v0:]], axis=1).astype(BF16)
    proj_w = w_in_ext.shape[1]
    scale = HEAD_DIM ** -0.5
    qk_w = jnp.concatenate([jnp.tile(q_norm_w.astype(F32) * scale, N_Q_HEADS),
                            jnp.tile(k_norm_w.astype(F32), 2 * N_KV_HEADS)]).reshape(1, -1)
    head_id = jnp.arange(2 * LANES) // HEAD_DIM
    blockdiag = jnp.where(head_id[:, None] == head_id[None, :], 1.0 / HEAD_DIM, 0.0).astype(BF16)
    cs, sel, ones_mask = _rope_tables(positions)
    row = lambda w: pl.BlockSpec((tm, w), lambda i: (i, 0))
    q_arr, kv_arr, u_arr = pl.pallas_call(
        _in_proj_kernel,
        grid=(n_row_tiles,),
        in_specs=[row(D), _full((1, D)), _full((D, proj_w)), _full((1, qk_w.shape[1])),
                  _full((2 * LANES, 2 * LANES)), row(2 * ROPE_HALF), _full(sel.shape), _full((1, LANES))],
        out_specs=[row(ATTN_WIDTH), row(4 * KV_WIDTH), row(POOL_WIDTH)],
        out_shape=[jax.ShapeDtypeStruct((T, ATTN_WIDTH), BF16),
                   jax.ShapeDtypeStruct((T, 4 * KV_WIDTH), BF16),
                   jax.ShapeDtypeStruct((T, POOL_WIDTH), F32)],
        compiler_params=pltpu.CompilerParams(dimension_semantics=("parallel",),
                                             vmem_limit_bytes=VMEM_LIMIT_BYTES),
        name="in_proj",
    )(x2, norm1_w.reshape(1, D).astype(F32), w_in_ext, qk_w, blockdiag, cs, sel, ones_mask)

    nq = S // tm
    sub_per_tile = tm // ATTN_BLOCK
    halo_per_tile = tm // POOL_HALO
    rr = jnp.arange(ATTN_BLOCK)[:, None] + POOL_LEAD
    cc = jnp.arange(ATTN_BLOCK + POOL_LEAD)[None, :]
    band = jnp.stack([((cc <= rr) & (cc > rr - w)) for w in POOL_WINDOWS]).astype(BF16)
    tile_spec = lambda w: pl.BlockSpec((tm, w), lambda b, i: (b * nq + i, 0))
    mixed = pl.pallas_call(
        _mixer_kernel,
        grid=(B, nq),
        in_specs=[pl.BlockSpec(memory_space=pltpu.SMEM),
                  tile_spec(ATTN_WIDTH), tile_spec(4 * KV_WIDTH),
                  pl.BlockSpec((ATTN_BLOCK, 4 * KV_WIDTH),
                               lambda b, i: (jnp.maximum((b * nq + i) * sub_per_tile - 1, 0), 0)),
                  tile_spec(POOL_WIDTH),
                  pl.BlockSpec((POOL_HALO, POOL_WIDTH),
                               lambda b, i: (jnp.maximum((b * nq + i) * halo_per_tile - 1, 0), 0)),
                  pl.BlockSpec(band.shape, lambda b, i: (0, 0, 0)),
                  pl.BlockSpec(pool_w.shape, lambda b, i: (0, 0, 0)),
                  pl.BlockSpec((1, POOL_WIDTH), lambda b, i: (0, 0))],
        out_specs=tile_spec(D),
        out_shape=jax.ShapeDtypeStruct((T, D), BF16),
        compiler_params=pltpu.CompilerParams(dimension_semantics=("parallel", "parallel"),
                                             vmem_limit_bytes=VMEM_LIMIT_BYTES),
        name="mixer",
    )(sinks.astype(F32), q_arr, kv_arr, kv_arr, u_arr, u_arr, band, pool_w.astype(BF16),
      pool_scale.reshape(1, POOL_WIDTH).astype(F32))

    wr_pad = jnp.zeros((D, LANES), F32).at[:, :N_EXPERTS].set(w_router.astype(F32))
    wr_hi, wr_lo = _split_bf16(wr_pad)
    br_pad = jnp.zeros((1, LANES), F32).at[0, :N_EXPERTS].set(b_router.astype(F32))
    tri = (jnp.arange(tm)[:, None] < jnp.arange(tm)[None, :]).astype(BF16)
    h_arr, n2_rows, route, gates, cnt = pl.pallas_call(
        _out_router_kernel,
        grid=(n_row_tiles,),
        in_specs=[row(D), row(D), _full((D, D)), _full((1, D)), _full((D, LANES)), _full((D, LANES)),
                  _full((1, LANES)), _full((tm, tm))],
        out_specs=[row(D), pl.BlockSpec((tm * SUBLANES, LANES), lambda i: (i, 0)),
                   pl.BlockSpec((2 * TOP_K, tm), lambda i: (0, i)),
                   pl.BlockSpec((2 * TOP_K, tm), lambda i: (0, i)),
                   _full((N_EXPERTS, LANES))],
        out_shape=[jax.ShapeDtypeStruct((T, D), F32),
                   jax.ShapeDtypeStruct((T * SUBLANES, LANES), F32),
                   jax.ShapeDtypeStruct((2 * TOP_K, T), jnp.int32),
                   jax.ShapeDtypeStruct((2 * TOP_K, T), F32),
                   jax.ShapeDtypeStruct((N_EXPERTS, LANES), F32)],
        scratch_shapes=[pltpu.VMEM((N_EXPERTS, LANES), F32)],
        compiler_params=pltpu.CompilerParams(dimension_semantics=("arbitrary",),
                                             vmem_limit_bytes=VMEM_LIMIT_BYTES),
        name="out_router",
    )(x2, mixed, w_out.astype(BF16), norm2_w.reshape(1, D).astype(F32), wr_hi, wr_lo, br_pad, tri)

    bm = EXPERT_BLOCK
    n_assign = T * TOP_K
    assert n_assign % bm == 0
    n_blk = n_assign // bm + N_EXPERTS
    n_slots = n_blk * bm
    counts = cnt[:, 0].astype(jnp.int32)
    padded = ((counts + bm - 1) // bm) * bm
    pend = jnp.cumsum(padded)
    pstart = pend - padded
    expert = route[:TOP_K]
    rank = route[TOP_K:]
    start_of = jnp.zeros_like(rank)
    for e in range(N_EXPERTS):
        start_of = jnp.where(expert == e, pstart[e], start_of)
    slot = (start_of + rank).reshape(-1)
    dst_real = (jnp.arange(TOP_K, dtype=jnp.int32)[:, None] * T
                + jnp.arange(T, dtype=jnp.int32)[None, :]).reshape(-1)
    pad_e = padded - counts
    lane_i = jnp.arange(bm, dtype=jnp.int32)[None, :]
    is_pad = lane_i < pad_e[:, None]
    spare = jnp.cumsum((~is_pad).reshape(-1).astype(jnp.int32)) - 1
    hole_key = jnp.where(is_pad, (pstart + counts)[:, None] + lane_i, pend[-1] + spare.reshape(N_EXPERTS, bm))
    hole_key = hole_key.reshape(-1)
    hole_dst = n_assign + ((hole_key // bm) % 2) * bm + hole_key % bm
    _, slot_dst = lax.sort((jnp.concatenate([slot, hole_key]), jnp.concatenate([dst_real, hole_dst])),
                           num_keys=1)
    slot_src = jnp.where(slot_dst < n_assign, slot_dst % T, 0)
    blk_start = jnp.arange(n_blk, dtype=jnp.int32) * bm
    blk_expert = jnp.minimum(jnp.sum((blk_start[:, None] >= pend[None, :]).astype(jnp.int32), axis=1),
                             N_EXPERTS - 1)
    n_used = (pend[-1] // bm).astype(jnp.int32).reshape(1)
    slot_src = slot_src.reshape(n_blk, 1, bm)
    slot_dst = slot_dst.reshape(n_blk, 1, bm)

    out_rows = (n_assign + 2 * bm) * SUBLANES
    blk_idx = lambda b, be, nu: (b, 0, 0)
    nxt_idx = lambda b, be, nu: (jnp.minimum(b + 1, n_blk - 1), 0, 0)
    exp_idx = lambda b, be, nu: (be[b], 0, 0)
    block_buf = pltpu.VMEM((bm * SUBLANES, LANES), F32)
    y_rows = pl.pallas_call(
        _moe_kernel,
        grid_spec=pltpu.PrefetchScalarGridSpec(
            num_scalar_prefetch=2,
            grid=(n_blk,),
            in_specs=[pl.BlockSpec((1, 1, bm), blk_idx, memory_space=pltpu.SMEM),
                      pl.BlockSpec((1, 1, bm), nxt_idx, memory_space=pltpu.SMEM),
                      pl.BlockSpec((1, 1, bm), blk_idx, memory_space=pltpu.SMEM),
                      pl.BlockSpec(memory_space=pl.ANY),
                      pl.BlockSpec((1, D, 2 * d_ff), exp_idx),
                      pl.BlockSpec((1, 1, 2 * d_ff), exp_idx),
                      pl.BlockSpec((1, d_ff, D), exp_idx),
                      pl.BlockSpec((1, 1, D), exp_idx)],
            out_specs=pl.BlockSpec(memory_space=pl.ANY),
            scratch_shapes=[block_buf, block_buf, block_buf, block_buf,
                            pltpu.SemaphoreType.DMA((2,)),
                            pltpu.SemaphoreType.DMA((2,))]),
        out_shape=jax.ShapeDtypeStruct((out_rows, LANES), F32),
        compiler_params=pltpu.CompilerParams(dimension_semantics=("arbitrary",),
                                             vmem_limit_bytes=VMEM_LIMIT_BYTES),
        name="moe",
    )(blk_expert, n_used, slot_src, slot_src, slot_dst, n2_rows,
      w_gu.astype(BF16), b_gu.reshape(N_EXPERTS, 1, 2 * d_ff).astype(F32),
      w_down.astype(BF16), b_down.reshape(N_EXPERTS, 1, D).astype(F32))

    yk = lambda k: pl.BlockSpec((tm * SUBLANES, LANES), lambda i: (k * n_row_tiles + i, 0))
    out = pl.pallas_call(
        _combine_kernel,
        grid=(n_row_tiles,),
        in_specs=[row(D), pl.BlockSpec((2 * TOP_K, tm), lambda i: (0, i)), yk(0), yk(1), yk(2), yk(3)],
        out_specs=row(D),
        out_shape=jax.ShapeDtypeStruct((T, D), F32),
        compiler_params=pltpu.CompilerParams(dimension_semantics=("parallel",),
                                             vmem_limit_bytes=VMEM_LIMIT_BYTES),
        name="combine",
    )(h_arr, gates, y_rows, y_rows, y_rows, y_rows)
    return out.reshape(B, S, D)
```
